```python
import jax, jax.numpy as jnp
from jax import lax
import numpy as np

D_MODEL = 2048
BATCH = 16
SEQ = 2048
DEPTH = 2

N_A = DEPTH // 2
N_B = DEPTH - N_A
N_DENSE = (DEPTH + 1) // 2
N_MOE = DEPTH // 2

GLA_HEADS = 4
GLA_KEY_DIM = D_MODEL // 2
GLA_VALUE_DIM = D_MODEL
GLA_HEAD_K = GLA_KEY_DIM // GLA_HEADS
GLA_HEAD_V = GLA_VALUE_DIM // GLA_HEADS
GLA_GATE_RANK = 16
GLA_GATE_NORMALIZER = 16.0
GLA_CHUNK = 64
GLA_IN_DIM = 2 * GLA_KEY_DIM + 2 * GLA_VALUE_DIM + GLA_GATE_RANK

MLA_NOPE_DIM = 128
MLA_ROPE_DIM = 64
MLA_V_DIM = 128
MLA_HEADS = D_MODEL // MLA_V_DIM
MLA_Q_RANK = 512
MLA_KV_RANK = 512
ROPE_THETA = 10000.0
Q_BLOCK = 128

D_FF = 5632
N_EXPERTS = 8
TOP_K = 2
D_EXPERT = 7 * D_MODEL // 2
MOE_BLOCK = 256
NORM_EPS = 1e-6

kernel_name = 'yoco_gla_mla_moe_hybrid'


def rms_norm(x, g):
    xf = x.astype(jnp.float32)
    y = xf * lax.rsqrt(jnp.mean(xf * xf, axis=-1, keepdims=True) + NORM_EPS)
    return (y * g.astype(jnp.float32)).astype(x.dtype)


def ada_modulation(c, w, b):
    mod = jax.nn.silu(c) @ w + b
    return jnp.split(mod[:, None, :], w.shape[-1] // D_MODEL, axis=-1)


def rope_tables(positions):
    inv_freq = ROPE_THETA ** (-jnp.arange(0, MLA_ROPE_DIM, 2, dtype=jnp.float32) / MLA_ROPE_DIM)
    ang = positions.astype(jnp.float32)[..., None] * inv_freq
    return jnp.cos(ang), jnp.sin(ang)


def apply_rope(t, cos, sin):
    t1, t2 = jnp.split(t.astype(jnp.float32), 2, axis=-1)
    return jnp.concatenate([t1 * cos - t2 * sin, t1 * sin + t2 * cos], axis=-1).astype(t.dtype)


def gla_chunked(q, k, v, log_a):
    B, H, T, DK = q.shape
    DV = v.shape[-1]
    N = T // GLA_CHUNK
    q = q.reshape(B, H, N, GLA_CHUNK, DK)
    k = k.reshape(B, H, N, GLA_CHUNK, DK)
    log_a = log_a.reshape(B, H, N, GLA_CHUNK, DK)
    v = v.reshape(B, H, N, GLA_CHUNK, DV)
    b = jnp.cumsum(log_a, axis=3)
    b_last = b[:, :, :, -1:, :]
    q_in = q * jnp.exp(b)
    k_in = k * jnp.exp(-b)
    k_st = k * jnp.exp(b_last - b)
    causal = jnp.tril(jnp.ones((GLA_CHUNK, GLA_CHUNK), dtype=bool))
    s = jnp.where(causal, jnp.einsum('bhnid,bhnjd->bhnij', q_in, k_in), 0.0)
    o_intra = jnp.einsum('bhnij,bhnjv->bhniv', s, v)
    decay = jnp.exp(b_last[:, :, :, 0, :])

    def step(state, inp):
        q_n, k_n, v_n, dec_n = inp
        o_n = jnp.einsum('bhcd,bhdv->bhcv', q_n, state)
        state = dec_n[..., None] * state + jnp.einsum('bhcd,bhcv->bhdv', k_n, v_n)
        return state, o_n

    s0 = jnp.zeros((B, H, DK, DV), q.dtype)
    xs = (jnp.moveaxis(q_in, 2, 0), jnp.moveaxis(k_st, 2, 0),
          jnp.moveaxis(v, 2, 0), jnp.moveaxis(decay, 2, 0))
    _, o_inter = lax.scan(step, s0, xs)
    o = o_intra + jnp.moveaxis(o_inter, 0, 2)
    return o.reshape(B, H, T, DV)


def gla_mixer(h, w_in, w_gk2, b_gk, norm_w, w_out):
    B, S, _ = h.shape
    proj = h @ w_in
    q, k, v, g, gk_low = jnp.split(
        proj, [GLA_KEY_DIM, 2 * GLA_KEY_DIM, 2 * GLA_KEY_DIM + GLA_VALUE_DIM,
               2 * GLA_KEY_DIM + 2 * GLA_VALUE_DIM], axis=-1)
    gate_logit = gk_low @ w_gk2 + b_gk
    log_a = jax.nn.log_sigmoid(gate_logit.astype(jnp.float32)) / GLA_GATE_NORMALIZER

    def heads(t, d):
        return t.reshape(B, S, GLA_HEADS, d).transpose(0, 2, 1, 3).astype(jnp.float32)

    o = gla_chunked(heads(q, GLA_HEAD_K) * GLA_HEAD_K ** -0.5, heads(k, GLA_HEAD_K),
                    heads(v, GLA_HEAD_V), heads(log_a, GLA_HEAD_K))
    o = rms_norm(o, norm_w)
    o = o.transpose(0, 2, 1, 3).reshape(B, S, GLA_VALUE_DIM).astype(h.dtype)
    return (jax.nn.silu(g) * o) @ w_out


def mla_shared_kv(x, c, ada_w, ada_b, norm_w, w_kv_a, kv_norm_w, w_kv_b, cos, sin):
    B, S, _ = x.shape
    shift, scale = ada_modulation(c, ada_w, ada_b)
    h = rms_norm(x, norm_w) * (1.0 + scale) + shift
    c_kv, k_r = jnp.split(h @ w_kv_a, [MLA_KV_RANK], axis=-1)
    kv = (rms_norm(c_kv, kv_norm_w) @ w_kv_b).reshape(B, S, MLA_HEADS, MLA_NOPE_DIM + MLA_V_DIM)
    k_nope, v = jnp.split(kv, [MLA_NOPE_DIM], axis=-1)
    k_rope = apply_rope(k_r, cos, sin)
    return k_nope, k_rope, v


def mla_attention(q_nope, q_rope, k_nope, k_rope, v):
    B, S, H, _ = q_nope.shape
    n_blk = S // Q_BLOCK
    scale = (MLA_NOPE_DIM + MLA_ROPE_DIM) ** -0.5
    qn = q_nope.reshape(B, n_blk, Q_BLOCK, H, MLA_NOPE_DIM).swapaxes(0, 1)
    qr = q_rope.reshape(B, n_blk, Q_BLOCK, H, MLA_ROPE_DIM).swapaxes(0, 1)
    k_pos = jnp.arange(S)

    def block(args):
        i, qn_b, qr_b = args
        s = (jnp.einsum('bqhd,bkhd->bhqk', qn_b, k_nope, preferred_element_type=jnp.float32)
             + jnp.einsum('bqhd,bkd->bhqk', qr_b, k_rope, preferred_element_type=jnp.float32)) * scale
        q_pos = i * Q_BLOCK + jnp.arange(Q_BLOCK)
        s = jnp.where(k_pos[None, :] <= q_pos[:, None], s, -jnp.inf)
        p = jax.nn.softmax(s, axis=-1).astype(v.dtype)
        return jnp.einsum('bhqk,bkhd->bqhd', p, v)

    o = lax.map(block, (jnp.arange(n_blk), qn, qr))
    return o.swapaxes(0, 1).reshape(B, S, H * MLA_V_DIM)


def mla_mixer(h, cos, sin, k_nope, k_rope, v, w_dq, q_norm_w, w_uq, w_out):
    B, S, _ = h.shape
    q = (rms_norm(h @ w_dq, q_norm_w) @ w_uq).reshape(B, S, MLA_HEADS, MLA_NOPE_DIM + MLA_ROPE_DIM)
    q_nope, q_rope = jnp.split(q, [MLA_NOPE_DIM], axis=-1)
    q_rope = apply_rope(q_rope, cos[:, :, None, :], sin[:, :, None, :])
    return mla_attention(q_nope, q_rope, k_nope, k_rope, v) @ w_out


def swiglu(h, w_gate, w_up, w_down):
    return (jax.nn.silu(h @ w_gate) * (h @ w_up)) @ w_down


def moe_swiglu(h, w_router, w_gate, w_up, w_down):
    B, S, D = h.shape
    T = B * S
    ht = h.reshape(T, D)
    logits = jnp.einsum('td,de->te', ht, w_router, preferred_element_type=jnp.float32)
    top_logit, top_idx = lax.top_k(logits, TOP_K)
    top_w = jax.nn.softmax(top_logit, axis=-1)
    A = T * TOP_K
    n_blocks = A // MOE_BLOCK + N_EXPERTS
    flat_e = top_idx.reshape(A)
    flat_tok = jnp.repeat(jnp.arange(T, dtype=jnp.int32), TOP_K)
    flat_w = top_w.reshape(A)
    order = jnp.argsort(flat_e)
    sorted_e = flat_e[order]
    counts = jnp.bincount(flat_e, length=N_EXPERTS)
    padded = (counts + MOE_BLOCK - 1) // MOE_BLOCK * MOE_BLOCK
    padded_end = jnp.cumsum(padded)
    padded_start = padded_end - padded
    group_start = jnp.cumsum(counts) - counts
    dest = padded_start[sorted_e] + jnp.arange(A) - group_start[sorted_e]
    buf_tok = jnp.full((n_blocks * MOE_BLOCK,), T, jnp.int32).at[dest].set(flat_tok[order])
    buf_w = jnp.zeros((n_blocks * MOE_BLOCK,), h.dtype).at[dest].set(flat_w[order].astype(h.dtype))
    block_start = jnp.arange(n_blocks) * MOE_BLOCK
    block_expert = jnp.minimum(jnp.sum(padded_end[None, :] <= block_start[:, None], axis=1),
                               N_EXPERTS - 1)
    h_pad = jnp.concatenate([ht, jnp.zeros((1, D), h.dtype)], axis=0)

    def expert_block(args):
        tok, e = args
        xb = h_pad[tok]
        return (jax.nn.silu(xb @ w_gate[e]) * (xb @ w_up[e])) @ w_down[e]

    y_blocks = lax.map(expert_block, (buf_tok.reshape(n_blocks, MOE_BLOCK), block_expert))
    y = jnp.zeros((T + 1, D), h.dtype).at[buf_tok].add(y_blocks.reshape(-1, D) * buf_w[:, None])
    return y[:T].reshape(B, S, D)


def setup_inputs(seed: int = 0) -> dict:
    key = jax.random.key(seed)
    ks = iter(jax.random.split(key, 40))

    def w(shape, fan_in):
        return jax.random.normal(next(ks), shape, jnp.float32) * (fan_in ** -0.5)

    def small(shape, s):
        return jax.random.normal(next(ks), shape, jnp.float32) * s

    def gain(shape):
        return 1.0 + 0.05 * jax.random.normal(next(ks), shape, jnp.float32)

    D = D_MODEL
    return {
        'x': jax.random.normal(next(ks), (BATCH, SEQ, D), jnp.float32),
        'c': jax.random.normal(next(ks), (BATCH, D), jnp.float32),
        'positions': jnp.broadcast_to(jnp.arange(SEQ, dtype=jnp.int32), (BATCH, SEQ)),
        'ada_mix_w': w((DEPTH, D, 3 * D), D),
        'ada_mix_b': small((DEPTH, 3 * D), 0.02),
        'norm_mix_pre': gain((DEPTH, D)),
        'norm_mix_post': gain((DEPTH, D)),
        'ada_ffn_w': w((DEPTH, D, 3 * D), D),
        'ada_ffn_b': small((DEPTH, 3 * D), 0.02),
        'norm_ffn_pre': gain((DEPTH, D)),
        'norm_ffn_post': gain((DEPTH, D)),
        'gla_w_in': w((N_A, D, GLA_IN_DIM), D),
        'gla_w_gk2': w((N_A, GLA_GATE_RANK, GLA_KEY_DIM), GLA_GATE_RANK),
        'gla_b_gk': small((N_A, GLA_KEY_DIM), 0.1),
        'gla_norm_w': gain((N_A, GLA_HEAD_V)),
        'gla_w_out': w((N_A, GLA_VALUE_DIM, D), GLA_VALUE_DIM),
        'kv_ada_w': w((D, 2 * D), D),
        'kv_ada_b': small((2 * D,), 0.02),
        'kv_norm_w': gain((D,)),
        'mla_w_kv_a': w((D, MLA_KV_RANK + MLA_ROPE_DIM), D),
        'mla_kv_norm_w': gain((MLA_KV_RANK,)),
        'mla_w_kv_b': w((MLA_KV_RANK, MLA_HEADS * (MLA_NOPE_DIM + MLA_V_DIM)), MLA_KV_RANK),
        'mla_w_dq': w((N_B, D, MLA_Q_RANK), D),
        'mla_q_norm_w': gain((N_B, MLA_Q_RANK)),
        'mla_w_uq': w((N_B, MLA_Q_RANK, MLA_HEADS * (MLA_NOPE_DIM + MLA_ROPE_DIM)), MLA_Q_RANK),
        'mla_w_out': w((N_B, MLA_HEADS * MLA_V_DIM, D), MLA_HEADS * MLA_V_DIM),
        'ffn_w_gate': w((N_DENSE, D, D_FF), D),
        'ffn_w_up': w((N_DENSE, D, D_FF), D),
        'ffn_w_down': w((N_DENSE, D_FF, D), D_FF),
        'moe_w_router': w((N_MOE, D, N_EXPERTS), D),
        'moe_w_gate': w((N_MOE, N_EXPERTS, D, D_EXPERT), D),
        'moe_w_up': w((N_MOE, N_EXPERTS, D, D_EXPERT), D),
        'moe_w_down': w((N_MOE, N_EXPERTS, D_EXPERT, D), D_EXPERT),
    }


def reference(x, c, positions, ada_mix_w, ada_mix_b, norm_mix_pre, norm_mix_post,
              ada_ffn_w, ada_ffn_b, norm_ffn_pre, norm_ffn_post,
              gla_w_in, gla_w_gk2, gla_b_gk, gla_norm_w, gla_w_out,
              kv_ada_w, kv_ada_b, kv_norm_w, mla_w_kv_a, mla_kv_norm_w, mla_w_kv_b,
              mla_w_dq, mla_q_norm_w, mla_w_uq, mla_w_out,
              ffn_w_gate, ffn_w_up, ffn_w_down,
              moe_w_router, moe_w_gate, moe_w_up, moe_w_down):
    cos, sin = rope_tables(positions)
    k_nope = k_rope = v_mla = None
    for layer in range(DEPTH):
        if layer == N_A:
            k_nope, k_rope, v_mla = mla_shared_kv(x, c, kv_ada_w, kv_ada_b, kv_norm_w,
                                                  mla_w_kv_a, mla_kv_norm_w, mla_w_kv_b, cos, sin)
        shift, scale, gate = ada_modulation(c, ada_mix_w[layer], ada_mix_b[layer])
        h = rms_norm(x, norm_mix_pre[layer]) * (1.0 + scale) + shift
        if layer < N_A:
            i = layer
            y = gla_mixer(h, gla_w_in[i], gla_w_gk2[i], gla_b_gk[i], gla_norm_w[i], gla_w_out[i])
        else:
            j = layer - N_A
            y = mla_mixer(h, cos, sin, k_nope, k_rope, v_mla,
                          mla_w_dq[j], mla_q_norm_w[j], mla_w_uq[j], mla_w_out[j])
        x = x + gate * rms_norm(y, norm_mix_post[layer])
        shift, scale, gate = ada_modulation(c, ada_ffn_w[layer], ada_ffn_b[layer])
        h = rms_norm(x, norm_ffn_pre[layer]) * (1.0 + scale) + shift
        if layer % 2 == 0:
            f = layer // 2
            y = swiglu(h, ffn_w_gate[f], ffn_w_up[f], ffn_w_down[f])
        else:
            m = layer // 2
            y = moe_swiglu(h, moe_w_router[m], moe_w_gate[m], moe_w_up[m], moe_w_down[m])
        x = x + gate * rms_norm(y, norm_ffn_post[layer])
    return x
```

```python
import functools

import jax
import jax.numpy as jnp
from jax import lax
from jax.experimental import pallas as pl
from jax.experimental.pallas import tpu as pltpu

F32 = jnp.float32
BF16 = jnp.bfloat16

NORM_EPS = 1e-6
GLA_HEADS = 4
GLA_GATE_RANK = 16
GLA_GATE_NORMALIZER = 16.0
GLA_CHUNK = 64
MLA_NOPE_DIM = 128
MLA_ROPE_DIM = 64
MLA_V_DIM = 128
ROPE_THETA = 10000.0
TOP_K = 2

LANES = 128
MLA_SLAB = 2 * LANES
V7X_VMEM_LIMIT_BYTES = 56 * 2**20

TM_PROJ = 512
TN_PROJ = 1024
TM_FFN = 512
TF_FFN = 512
TM_OUT = 512
TM_MLA = 256
TQ_ATT = 512
L_GLA = 256
TM_ROUTE = 512
TM_MOE = 1024
TF_MOE = 512
TN_ADA = 512


def _cparams(*sem):
    return pltpu.CompilerParams(dimension_semantics=sem, vmem_limit_bytes=V7X_VMEM_LIMIT_BYTES)


def _tile(n, pref):
    t = min(n, pref)
    assert n % t == 0, (n, pref)
    return t


def _dot(a, b):
    return jnp.dot(a, b, preferred_element_type=F32)


def _dot_nt(a, b):
    return lax.dot_general(a, b, (((1,), (1,)), ((), ())), preferred_element_type=F32)


def _dot_tn(a, b):
    return lax.dot_general(a, b, (((0,), (0,)), ((), ())), preferred_element_type=F32)


def _silu(x):
    return x / (1.0 + jnp.exp(-x))


def _rms(x):
    return lax.rsqrt(jnp.mean(x * x, axis=-1, keepdims=True) + NORM_EPS)


def _norm_mod(x, g, scale, shift):
    return (x * _rms(x) * g) * (1.0 + scale) + shift


def _split3(x):
    hi = x.astype(BF16)
    r1 = x - hi.astype(F32)
    mid = r1.astype(BF16)
    lo = (r1 - mid.astype(F32)).astype(BF16)
    return hi, mid, lo


def _ada_kernel(c_ref, w_ref, b_ref, o_ref):
    s = _silu(c_ref[...])
    o_ref[...] = jnp.dot(s, w_ref[...], preferred_element_type=F32,
                         precision=lax.Precision.HIGHEST) + b_ref[...]


def _ada_mod(c, w, b):
    L, D, N = w.shape
    B = c.shape[0]
    tn = _tile(N, TN_ADA)
    return pl.pallas_call(
        _ada_kernel,
        grid=(L, N // tn),
        in_specs=[pl.BlockSpec((B, D), lambda l, j: (0, 0)),
                  pl.BlockSpec((None, D, tn), lambda l, j: (l, 0, j)),
                  pl.BlockSpec((None, 1, tn), lambda l, j: (l, 0, j))],
        out_specs=pl.BlockSpec((None, B, tn), lambda l, j: (l, 0, j)),
        out_shape=jax.ShapeDtypeStruct((L, B, N), F32),
        compiler_params=_cparams("parallel", "parallel"),
        name="ada_mod",
    )(c, w, b.reshape(L, 1, N))


def _mods(mod, n):
    B = mod.shape[0]
    return [m.reshape(B, 1, -1) for m in jnp.split(mod, n, axis=-1)]


def _batch_spec(D, rows_per_batch, tm):
    return pl.BlockSpec((None, 1, D), lambda i, *_: ((i * tm) // rows_per_batch, 0, 0))


def _const_spec(shape):
    nd = len(shape)
    return pl.BlockSpec(shape, lambda *_: (0,) * nd)


def _gla_proj_kernel(x_ref, sc_ref, sh_ref, g_ref, w_ref, wg1_ref, wg2_ref, bgk_ref,
                     proj_ref, la_ref, h_ref):
    @pl.when(pl.program_id(1) == 0)
    def _():
        hb = _norm_mod(x_ref[...], g_ref[...], sc_ref[...], sh_ref[...]).astype(BF16)
        h_ref[...] = hb
        gk_low = _dot(hb, wg1_ref[...])
        logit = _dot(gk_low.astype(BF16), wg2_ref[...]) + bgk_ref[...]
        log_sig = jnp.minimum(logit, 0.0) - jnp.log1p(jnp.exp(-jnp.abs(logit)))
        la_ref[...] = log_sig * (1.0 / GLA_GATE_NORMALIZER)

    proj_ref[...] = _dot(h_ref[...], w_ref[...]).astype(BF16)


def _gla_proj(x, scale, shift, g, w_main, w_gk1, w_gk2, b_gk, S):
    T, D = x.shape
    NM = w_main.shape[1]
    KD = w_gk2.shape[1]
    tm = _tile(S, TM_PROJ)
    tn = _tile(NM, TN_PROJ)
    return pl.pallas_call(
        _gla_proj_kernel,
        grid=(T // tm, NM // tn),
        in_specs=[pl.BlockSpec((tm, D), lambda i, j: (i, 0)),
                  _batch_spec(D, S, tm), _batch_spec(D, S, tm),
                  _const_spec((1, D)),
                  pl.BlockSpec((D, tn), lambda i, j: (0, j)),
                  _const_spec((D, LANES)),
                  _const_spec((LANES, KD)),
                  _const_spec((1, KD))],
        out_specs=[pl.BlockSpec((tm, tn), lambda i, j: (i, j)),
                   pl.BlockSpec((tm, KD), lambda i, j: (i, 0))],
        out_shape=[jax.ShapeDtypeStruct((T, NM), BF16), jax.ShapeDtypeStruct((T, KD), F32)],
        scratch_shapes=[pltpu.VMEM((tm, D), BF16)],
        compiler_params=_cparams("parallel", "arbitrary"),
        name="gla_proj",
    )(x, scale, shift, g, w_main, w_gk1, w_gk2, b_gk)


def _gla_kernel(q_ref, k_ref, v_ref, g_ref, la_ref, nw_ref, o_ref, st_ref, *, L, C, DK):
    @pl.when(pl.program_id(2) == 0)
    def _():
        st_ref[...] = jnp.zeros_like(st_ref)

    row = lax.broadcasted_iota(jnp.int32, (L, L), 0)
    col = lax.broadcasted_iota(jnp.int32, (L, L), 1)
    tri = ((row & -C) == (col & -C)) & (col <= row)
    trib = jnp.where(tri, 1.0, 0.0).astype(BF16)

    la_hi, la_mid, la_lo = _split3(la_ref[...])
    b = _dot(trib, la_hi) + _dot(trib, la_mid) + _dot(trib, la_lo)
    n_chunks = L // C
    b_last = [b[c * C + C - 1:c * C + C, :] for c in range(n_chunks)]
    bl = jnp.concatenate([jnp.broadcast_to(r, (C, DK)) for r in b_last], axis=0)

    q = q_ref[...].astype(F32) * (DK ** -0.5)
    k = k_ref[...].astype(F32)
    q_in = (q * jnp.exp(b)).astype(BF16)
    k_in = (k * jnp.exp(-b)).astype(BF16)
    k_st = (k * jnp.exp(bl - b)).astype(BF16)
    v = v_ref[...]

    s = jnp.where(tri, _dot_nt(q_in, k_in), 0.0)
    o_intra = _dot(s.astype(BF16), v)

    st = st_ref[...]
    outs = []
    for c in range(n_chunks):
        sl = slice(c * C, (c + 1) * C)
        outs.append(o_intra[sl] + _dot_nt(q_in[sl], st.astype(BF16)))
        st = st * jnp.exp(b_last[c]) + _dot_tn(v[sl], k_st[sl])
    st_ref[...] = st

    o = jnp.concatenate(outs, axis=0)
    o = o * _rms(o) * nw_ref[...]
    o_ref[...] = (_silu(g_ref[...].astype(F32)) * o).astype(BF16)


def _gla_mix(proj, la, norm_w, B, S, KD, VD):
    T = proj.shape[0]
    H = GLA_HEADS
    DK, DV = KD // H, VD // H
    L = _tile(S, L_GLA)
    nL = S // L
    assert (2 * KD) % DV == 0 and L % GLA_CHUNK == 0
    k_off, v_off, g_off = KD // DK, (2 * KD) // DV, (2 * KD + VD) // DV
    rows = lambda b, h, n: b * nL + n
    return pl.pallas_call(
        functools.partial(_gla_kernel, L=L, C=GLA_CHUNK, DK=DK),
        grid=(B, H, nL),
        in_specs=[pl.BlockSpec((L, DK), lambda b, h, n: (rows(b, h, n), h)),
                  pl.BlockSpec((L, DK), lambda b, h, n: (rows(b, h, n), k_off + h)),
                  pl.BlockSpec((L, DV), lambda b, h, n: (rows(b, h, n), v_off + h)),
                  pl.BlockSpec((L, DV), lambda b, h, n: (rows(b, h, n), g_off + h)),
                  pl.BlockSpec((L, DK), lambda b, h, n: (rows(b, h, n), h)),
                  _const_spec((1, DV))],
        out_specs=pl.BlockSpec((L, DV), lambda b, h, n: (rows(b, h, n), h)),
        out_shape=jax.ShapeDtypeStruct((T, VD), BF16),
        scratch_shapes=[pltpu.VMEM((DV, DK), F32)],
        compiler_params=_cparams("parallel", "parallel", "arbitrary"),
        name="gla_mix",
    )(proj, proj, proj, proj, la, norm_w)


def _out_proj_kernel(a_ref, w_ref, x_ref, gate_ref, gp_ref, o_ref):
    y = _dot(a_ref[...], w_ref[...])
    o_ref[...] = x_ref[...] + gate_ref[...] * (y * _rms(y) * gp_ref[...])


def _out_proj(a, w, x, gate, g_post, S):
    T, K = a.shape
    D = w.shape[1]
    tm = _tile(S, TM_OUT)
    return pl.pallas_call(
        _out_proj_kernel,
        grid=(T // tm,),
        in_specs=[pl.BlockSpec((tm, K), lambda i: (i, 0)),
                  _const_spec((K, D)),
                  pl.BlockSpec((tm, D), lambda i: (i, 0)),
                  _batch_spec(D, S, tm),
                  _const_spec((1, D))],
        out_specs=pl.BlockSpec((tm, D), lambda i: (i, 0)),
        out_shape=jax.ShapeDtypeStruct((T, D), F32),
        compiler_params=_cparams("parallel"),
        name="out_proj",
    )(a, w, x, gate, g_post)


def _post_norm_kernel(y_ref, x_ref, gate_ref, gp_ref, o_ref):
    y = y_ref[...]
    o_ref[...] = x_ref[...] + gate_ref[...] * (y * _rms(y) * gp_ref[...])


def _post_norm(y, x, gate, g_post, S):
    T, D = x.shape
    tm = _tile(S, TM_OUT)
    return pl.pallas_call(
        _post_norm_kernel,
        grid=(T // tm,),
        in_specs=[pl.BlockSpec((tm, D), lambda i: (i, 0)),
                  pl.BlockSpec((tm, D), lambda i: (i, 0)),
                  _batch_spec(D, S, tm),
                  _const_spec((1, D))],
        out_specs=pl.BlockSpec((tm, D), lambda i: (i, 0)),
        out_shape=jax.ShapeDtypeStruct((T, D), F32),
        compiler_params=_cparams("parallel"),
        name="post_norm",
    )(y, x, gate, g_post)


def _ffn_kernel(x_ref, sc_ref, sh_ref, g_ref, wg_ref, wu_ref, wd_ref, gate_ref, gp_ref,
                o_ref, h_ref, acc_ref):
    f = pl.program_id(1)

    @pl.when(f == 0)
    def _():
        h_ref[...] = _norm_mod(x_ref[...], g_ref[...], sc_ref[...], sh_ref[...]).astype(BF16)
        acc_ref[...] = jnp.zeros_like(acc_ref)

    h = h_ref[...]
    act = (_silu(_dot(h, wg_ref[...])) * _dot(h, wu_ref[...])).astype(BF16)
    acc_ref[...] += _dot(act, wd_ref[...])

    @pl.when(f == pl.num_programs(1) - 1)
    def _():
        y = acc_ref[...]
        o_ref[...] = x_ref[...] + gate_ref[...] * (y * _rms(y) * gp_ref[...])


def _ffn(x, scale, shift, g_pre, wg, wu, wd, gate, g_post, S):
    T, D = x.shape
    F = wg.shape[1]
    tm = _tile(S, TM_FFN)
    tf = _tile(F, TF_FFN)
    return pl.pallas_call(
        _ffn_kernel,
        grid=(T // tm, F // tf),
        in_specs=[pl.BlockSpec((tm, D), lambda i, f: (i, 0)),
                  _batch_spec(D, S, tm), _batch_spec(D, S, tm),
                  _const_spec((1, D)),
                  pl.BlockSpec((D, tf), lambda i, f: (0, f)),
                  pl.BlockSpec((D, tf), lambda i, f: (0, f)),
                  pl.BlockSpec((tf, D), lambda i, f: (f, 0)),
                  _batch_spec(D, S, tm),
                  _const_spec((1, D))],
        out_specs=pl.BlockSpec((tm, D), lambda i, f: (i, 0)),
        out_shape=jax.ShapeDtypeStruct((T, D), F32),
        scratch_shapes=[pltpu.VMEM((tm, D), BF16), pltpu.VMEM((tm, D), F32)],
        compiler_params=_cparams("parallel", "arbitrary"),
        name="ffn",
    )(x, scale, shift, g_pre, wg, wu, wd, gate, g_post)


def _store_slabs(out_ref, nope, rope, n_heads):
    for h in range(n_heads):
        out_ref[:, h * MLA_SLAB:h * MLA_SLAB + LANES] = nope[:, h * LANES:(h + 1) * LANES].astype(BF16)
        r = rope if rope.shape[1] == LANES else rope[:, h * LANES:(h + 1) * LANES]
        out_ref[:, h * MLA_SLAB + LANES:(h + 1) * MLA_SLAB] = r.astype(BF16)


def _kv_proj_kernel(x_ref, sc_ref, sh_ref, g_ref, wa_ref, war_ref, g2_ref, wbk_ref, wbv_ref,
                    cos_ref, sin_ref, k_ref, v_ref, *, n_heads):
    h = _norm_mod(x_ref[...], g_ref[...], sc_ref[...], sh_ref[...]).astype(BF16)
    ckv = _dot(h, wa_ref[...])
    kr = _dot(h, war_ref[...])
    cn = (ckv * _rms(ckv) * g2_ref[...]).astype(BF16)
    k_rope = kr[:, :LANES] * cos_ref[...] + kr[:, LANES:] * sin_ref[...]
    _store_slabs(k_ref, _dot(cn, wbk_ref[...]), k_rope, n_heads)
    v_ref[...] = _dot(cn, wbv_ref[...]).astype(BF16)


def _kv_proj(x, scale, shift, g, wa, war, g2, wbk, wbv, cos_t, sin_t, S):
    T, D = x.shape
    R = wa.shape[1]
    HV = wbk.shape[1]
    n_heads = HV // LANES
    tm = _tile(S, TM_MLA)
    row = lambda n: pl.BlockSpec((tm, n), lambda i: (i, 0))
    return pl.pallas_call(
        functools.partial(_kv_proj_kernel, n_heads=n_heads),
        grid=(T // tm,),
        in_specs=[row(D), _batch_spec(D, S, tm), _batch_spec(D, S, tm), _const_spec((1, D)),
                  _const_spec((D, R)), _const_spec((D, 2 * LANES)), _const_spec((1, R)),
                  _const_spec((R, HV)), _const_spec((R, HV)), row(LANES), row(LANES)],
        out_specs=[row(n_heads * MLA_SLAB), row(HV)],
        out_shape=[jax.ShapeDtypeStruct((T, n_heads * MLA_SLAB), BF16),
                   jax.ShapeDtypeStruct((T, HV), BF16)],
        compiler_params=_cparams("parallel"),
        name="kv_proj",
    )(x, scale, shift, g, wa, war, g2, wbk, wbv, cos_t, sin_t)


def _q_proj_kernel(x_ref, sc_ref, sh_ref, g_ref, wdq_ref, gq_ref, wu_ref, cos_ref, sin_ref,
                   q_ref, *, n_heads):
    h = _norm_mod(x_ref[...], g_ref[...], sc_ref[...], sh_ref[...]).astype(BF16)
    ql = _dot(h, wdq_ref[...])
    qn = (ql * _rms(ql) * gq_ref[...]).astype(BF16)
    qa = _dot(qn, wu_ref[...])
    HV = n_heads * LANES
    cos_t, sin_t = cos_ref[...], sin_ref[...]
    for hd in range(n_heads):
        sl = slice(hd * LANES, (hd + 1) * LANES)
        q_ref[:, hd * MLA_SLAB:hd * MLA_SLAB + LANES] = qa[:, sl].astype(BF16)
        rope = qa[:, HV:2 * HV][:, sl] * cos_t + qa[:, 2 * HV:][:, sl] * sin_t
        q_ref[:, hd * MLA_SLAB + LANES:(hd + 1) * MLA_SLAB] = rope.astype(BF16)


def _q_proj(x, scale, shift, g, wdq, gq, wu, cos_t, sin_t, S):
    T, D = x.shape
    R = wdq.shape[1]
    n_heads = wu.shape[1] // (3 * LANES)
    tm = _tile(S, TM_MLA)
    row = lambda n: pl.BlockSpec((tm, n), lambda i: (i, 0))
    return pl.pallas_call(
        functools.partial(_q_proj_kernel, n_heads=n_heads),
        grid=(T // tm,),
        in_specs=[row(D), _batch_spec(D, S, tm), _batch_spec(D, S, tm), _const_spec((1, D)),
                  _const_spec((D, R)), _const_spec((1, R)), _const_spec((R, 3 * n_heads * LANES)),
                  row(LANES), row(LANES)],
        out_specs=row(n_heads * MLA_SLAB),
        out_shape=jax.ShapeDtypeStruct((T, n_heads * MLA_SLAB), BF16),
        compiler_params=_cparams("parallel"),
        name="q_proj",
    )(x, scale, shift, g, wdq, gq, wu, cos_t, sin_t)


def _attn_kernel(q_ref, k_ref, v_ref, o_ref, *, tq, scale):
    i = pl.program_id(2)
    q = q_ref[...]

    def step(j, carry, masked):
        m, l, acc = carry
        k = k_ref[pl.ds(pl.multiple_of(j * tq, tq), tq), :]
        v = v_ref[pl.ds(pl.multiple_of(j * tq, tq), tq), :]
        s = _dot_nt(q, k) * scale
        if masked:
            row = lax.broadcasted_iota(jnp.int32, (tq, tq), 0)
            col = lax.broadcasted_iota(jnp.int32, (tq, tq), 1)
            s = jnp.where(col <= row, s, -jnp.inf)
        m_new = jnp.maximum(m, jnp.max(s, axis=-1, keepdims=True))
        alpha = jnp.exp(m - m_new)
        p = jnp.exp(s - m_new)
        l = alpha * l + jnp.sum(p, axis=-1, keepdims=True)
        acc = alpha * acc + _dot(p.astype(BF16), v)
        return m_new, l, acc

    init = (jnp.full((tq, 1), -jnp.inf, F32), jnp.zeros((tq, 1), F32),
            jnp.zeros((tq, v_ref.shape[1]), F32))
    carry = lax.fori_loop(0, i, lambda j, c: step(j, c, False), init)
    _, l, acc = step(i, carry, True)
    o_ref[...] = (acc / l).astype(BF16)


def _attention(q, k, v, B, S):
    T = q.shape[0]
    H = q.shape[1] // MLA_SLAB
    tq = _tile(S, TQ_ATT)
    nq = S // tq
    scale = (MLA_NOPE_DIM + MLA_ROPE_DIM) ** -0.5
    return pl.pallas_call(
        functools.partial(_attn_kernel, tq=tq, scale=scale),
        grid=(B, H, nq),
        in_specs=[pl.BlockSpec((tq, MLA_SLAB), lambda b, h, i: (b * nq + i, h)),
                  pl.BlockSpec((S, MLA_SLAB), lambda b, h, i: (b, h)),
                  pl.BlockSpec((S, MLA_V_DIM), lambda b, h, i: (b, h))],
        out_specs=pl.BlockSpec((tq, MLA_V_DIM), lambda b, h, i: (b * nq + i, h)),
        out_shape=jax.ShapeDtypeStruct((T, H * MLA_V_DIM), BF16),
        compiler_params=_cparams("parallel", "parallel", "arbitrary"),
        name="mla_attention",
    )(q, k, v)


def _router_kernel(x_ref, sc_ref, sh_ref, g_ref, whi_ref, wlo_ref, h_ref, route_ref, cnt_ref,
                   carry_ref, *, n_experts):
    @pl.when(pl.program_id(0) == 0)
    def _():
        carry_ref[...] = jnp.zeros_like(carry_ref)

    h = _norm_mod(x_ref[...], g_ref[...], sc_ref[...], sh_ref[...])
    hb = h.astype(BF16)
    h_ref[...] = hb
    h_lo = (h - hb.astype(F32)).astype(BF16)
    logits = _dot(hb, whi_ref[...]) + (_dot(hb, wlo_ref[...]) + _dot(h_lo, whi_ref[...]))

    tm = logits.shape[0]
    lane = lax.broadcasted_iota(jnp.int32, logits.shape, 1)
    logits = jnp.where(lane < n_experts, logits, -jnp.inf)
    m1 = jnp.max(logits, axis=-1, keepdims=True)
    i1 = jnp.min(jnp.where(logits == m1, lane, LANES), axis=-1, keepdims=True)
    rest = jnp.where(lane == i1, -jnp.inf, logits)
    m2 = jnp.max(rest, axis=-1, keepdims=True)
    i2 = jnp.min(jnp.where(rest == m2, lane, LANES), axis=-1, keepdims=True)
    e2 = jnp.exp(m2 - m1)
    w1 = 1.0 / (1.0 + e2)
    w2 = e2 / (1.0 + e2)

    oh1 = lane == i1
    oh2 = lane == i2
    onehot = jnp.where(oh1 | oh2, 1.0, 0.0)
    row = lax.broadcasted_iota(jnp.int32, (tm, tm), 0)
    col = lax.broadcasted_iota(jnp.int32, (tm, tm), 1)
    before = jnp.where(col < row, 1.0, 0.0).astype(BF16)
    rank = _dot(before, onehot.astype(BF16)) + carry_ref[...]
    p1 = jnp.sum(jnp.where(oh1, rank, 0.0), axis=-1, keepdims=True)
    p2 = jnp.sum(jnp.where(oh2, rank, 0.0), axis=-1, keepdims=True)
    carry_ref[...] += jnp.sum(onehot, axis=0, keepdims=True)
    cnt_ref[...] = carry_ref[...]

    fields = (i1.astype(F32), i2.astype(F32), w1, w2, p1, p2)
    out = jnp.zeros(logits.shape, F32)
    for n, val in enumerate(fields):
        out = jnp.where(lane == n, val, out)
    route_ref[...] = out


def _router(x, scale, shift, g, w_hi, w_lo, n_experts, S):
    T, D = x.shape
    tm = _tile(S, TM_ROUTE)
    return pl.pallas_call(
        functools.partial(_router_kernel, n_experts=n_experts),
        grid=(T // tm,),
        in_specs=[pl.BlockSpec((tm, D), lambda i: (i, 0)),
                  _batch_spec(D, S, tm), _batch_spec(D, S, tm), _const_spec((1, D)),
                  _const_spec((D, LANES)), _const_spec((D, LANES))],
        out_specs=[pl.BlockSpec((tm, D), lambda i: (i, 0)),
                   pl.BlockSpec((tm, LANES), lambda i: (i, 0)),
                   _const_spec((1, LANES))],
        out_shape=[jax.ShapeDtypeStruct((T, D), BF16), jax.ShapeDtypeStruct((T, LANES), F32),
                   jax.ShapeDtypeStruct((1, LANES), F32)],
        scratch_shapes=[pltpu.VMEM((1, LANES), F32)],
        compiler_params=_cparams("arbitrary"),
        name="moe_router",
    )(x, scale, shift, g, w_hi, w_lo)


def _moe_kernel(be_ref, nv_ref, x_ref, wg_ref, wu_ref, wd_ref, o_ref, acc_ref):
    i = pl.program_id(0)
    f = pl.program_id(1)
    last = pl.num_programs(1) - 1
    valid = i < nv_ref[0]

    @pl.when(f == 0)
    def _():
        acc_ref[...] = jnp.zeros_like(acc_ref)

    @pl.when(valid)
    def _():
        x = x_ref[...]
        act = (_silu(_dot(x, wg_ref[...])) * _dot(x, wu_ref[...])).astype(BF16)
        acc_ref[...] += _dot(act, wd_ref[...])

    @pl.when(f == last)
    def _():
        o_ref[...] = acc_ref[...].astype(o_ref.dtype)


def _moe_blocks(xs, block_expert, n_valid, wg, wu, wd, tm):
    R, D = xs.shape
    E, _, F = wg.shape
    tf = _tile(F, TF_MOE)
    nf = F // tf

    def fidx(i, f, nv):
        return jnp.where(i < nv[0], f, nf - 1)

    grid_spec = pltpu.PrefetchScalarGridSpec(
        num_scalar_prefetch=2,
        grid=(R // tm, nf),
        in_specs=[pl.BlockSpec((tm, D), lambda i, f, be, nv: (i, 0)),
                  pl.BlockSpec((None, D, tf), lambda i, f, be, nv: (be[i], 0, fidx(i, f, nv))),
                  pl.BlockSpec((None, D, tf), lambda i, f, be, nv: (be[i], 0, fidx(i, f, nv))),
                  pl.BlockSpec((None, tf, D), lambda i, f, be, nv: (be[i], fidx(i, f, nv), 0))],
        out_specs=pl.BlockSpec((tm, D), lambda i, f, be, nv: (i, 0)),
        scratch_shapes=[pltpu.VMEM((tm, D), F32)],
    )
    return pl.pallas_call(
        _moe_kernel,
        grid_spec=grid_spec,
        out_shape=jax.ShapeDtypeStruct((R, D), BF16),
        compiler_params=_cparams("parallel", "arbitrary"),
        name="moe_experts",
    )(block_expert, n_valid, xs, wg, wu, wd)


def _rope_tables(positions):
    half = MLA_ROPE_DIM // 2
    inv_freq = ROPE_THETA ** (-jnp.arange(0, MLA_ROPE_DIM, 2, dtype=F32) / MLA_ROPE_DIM)
    ang = positions.astype(F32).reshape(-1, 1) * inv_freq
    cos, sin = jnp.cos(ang), jnp.sin(ang)
    zeros = jnp.zeros((ang.shape[0], LANES - 2 * half), F32)
    return (jnp.concatenate([cos, cos, zeros], axis=1),
            jnp.concatenate([-sin, sin, zeros], axis=1))


def _swap_halves(w):
    half = w.shape[-1] // 2
    return jnp.concatenate([w[..., half:], w[..., :half]], axis=-1)


def _pad_lanes(w):
    return jnp.pad(w, [(0, 0)] * (w.ndim - 1) + [(0, LANES - w.shape[-1])])


def kernel(x, c, positions, ada_mix_w, ada_mix_b, norm_mix_pre, norm_mix_post, ada_ffn_w, ada_ffn_b, norm_ffn_pre, norm_ffn_post, gla_w_in, gla_w_gk2, gla_b_gk, gla_norm_w, gla_w_out, kv_ada_w, kv_ada_b, kv_norm_w, mla_w_kv_a, mla_kv_norm_w, mla_w_kv_b, mla_w_dq, mla_q_norm_w, mla_w_uq, mla_w_out, ffn_w_gate, ffn_w_up, ffn_w_down, moe_w_router, moe_w_gate, moe_w_up, moe_w_down):
    B, S, D = x.shape
    T = B * S
    xt = x.reshape(T, D)
    row = lambda a: a.reshape(1, -1)

    mix_mod = _ada_mod(c, ada_mix_w, ada_mix_b)
    ffn_mod = _ada_mod(c, ada_ffn_w, ada_ffn_b)
    kv_mod = _ada_mod(c, kv_ada_w[None], kv_ada_b[None])[0]
    cos_t, sin_t = _rope_tables(positions)

    shift, scale, gate = _mods(mix_mod[0], 3)
    KD = gla_w_gk2.shape[-1]
    VD = gla_w_out.shape[1]
    w_in = gla_w_in[0]
    n_main = 2 * KD + 2 * VD
    w_gk2 = jnp.pad(gla_w_gk2[0], ((0, LANES - GLA_GATE_RANK), (0, 0))).astype(BF16)
    proj, log_a = _gla_proj(xt, scale, shift, row(norm_mix_pre[0]),
                            w_in[:, :n_main].astype(BF16), _pad_lanes(w_in[:, n_main:]).astype(BF16),
                            w_gk2, row(gla_b_gk[0]), S)
    a = _gla_mix(proj, log_a, row(gla_norm_w[0]), B, S, KD, VD)
    xt = _out_proj(a, gla_w_out[0].astype(BF16), xt, gate, row(norm_mix_post[0]), S)

    shift, scale, gate = _mods(ffn_mod[0], 3)
    xt = _ffn(xt, scale, shift, row(norm_ffn_pre[0]), ffn_w_gate[0].astype(BF16),
              ffn_w_up[0].astype(BF16), ffn_w_down[0].astype(BF16), gate, row(norm_ffn_post[0]), S)

    kv_shift, kv_scale = _mods(kv_mod, 2)
    R_kv = mla_kv_norm_w.shape[0]
    H = D // MLA_V_DIM
    w_c = mla_w_kv_a[:, :R_kv]
    w_r = mla_w_kv_a[:, R_kv:]
    war = jnp.concatenate([_pad_lanes(w_r), _pad_lanes(_swap_halves(w_r))], axis=1)
    w_kvb = mla_w_kv_b.reshape(R_kv, H, MLA_NOPE_DIM + MLA_V_DIM)
    wbk = w_kvb[:, :, :MLA_NOPE_DIM].reshape(R_kv, H * MLA_NOPE_DIM)
    wbv = w_kvb[:, :, MLA_NOPE_DIM:].reshape(R_kv, H * MLA_V_DIM)
    k_cat, v_all = _kv_proj(xt, kv_scale, kv_shift, row(kv_norm_w), w_c.astype(BF16), war.astype(BF16),
                            row(mla_kv_norm_w), wbk.astype(BF16), wbv.astype(BF16), cos_t, sin_t, S)

    shift, scale, gate = _mods(mix_mod[1], 3)
    R_q = mla_q_norm_w.shape[1]
    w_uq = mla_w_uq[0].reshape(R_q, H, MLA_NOPE_DIM + MLA_ROPE_DIM)
    w_qn = w_uq[:, :, :MLA_NOPE_DIM].reshape(R_q, H * MLA_NOPE_DIM)
    w_qr = w_uq[:, :, MLA_NOPE_DIM:]
    wu = jnp.concatenate([w_qn, _pad_lanes(w_qr).reshape(R_q, H * LANES),
                          _pad_lanes(_swap_halves(w_qr)).reshape(R_q, H * LANES)], axis=1)
    q_cat = _q_proj(xt, scale, shift, row(norm_mix_pre[1]), mla_w_dq[0].astype(BF16),
                    row(mla_q_norm_w[0]), wu.astype(BF16), cos_t, sin_t, S)
    att = _attention(q_cat, k_cat, v_all, B, S)
    xt = _out_proj(att, mla_w_out[0].astype(BF16), xt, gate, row(norm_mix_post[1]), S)

    shift, scale, gate = _mods(ffn_mod[1], 3)
    E = moe_w_router.shape[-1]
    w_router = _pad_lanes(moe_w_router[0])
    w_hi = w_router.astype(BF16)
    w_lo = (w_router - w_hi.astype(F32)).astype(BF16)
    hm, route, counts = _router(xt, scale, shift, row(norm_ffn_pre[1]), w_hi, w_lo, E, S)

    tm = min(TM_MOE, T)
    n_assign = T * TOP_K
    n_blocks = n_assign // tm + E
    counts = counts[0, :E].astype(jnp.int32)
    padded = (counts + tm - 1) // tm * tm
    padded_end = jnp.cumsum(padded)
    padded_start = padded_end - padded
    e1, e2 = route[:, 0].astype(jnp.int32), route[:, 1].astype(jnp.int32)
    d1 = padded_start[e1] + route[:, 4].astype(jnp.int32)
    d2 = padded_start[e2] + route[:, 5].astype(jnp.int32)
    tok = jnp.arange(T, dtype=jnp.int32)
    buf_tok = jnp.zeros((n_blocks * tm,), jnp.int32).at[d1].set(tok).at[d2].set(tok)
    block_start = jnp.arange(n_blocks, dtype=jnp.int32) * tm
    block_expert = jnp.minimum(jnp.sum(padded_end[None, :] <= block_start[:, None], axis=1), E - 1)
    n_valid = (padded_end[-1] // tm).reshape(1)
    xs = jnp.take(hm, buf_tok, axis=0)
    yb = _moe_blocks(xs, block_expert.astype(jnp.int32), n_valid.astype(jnp.int32),
                     moe_w_gate[0].astype(BF16), moe_w_up[0].astype(BF16), moe_w_down[0].astype(BF16), tm)
    y = (jnp.take(yb, d1, axis=0).astype(F32) * route[:, 2:3]
         + jnp.take(yb, d2, axis=0).astype(F32) * route[:, 3:4])
    xt = _post_norm(y, xt, gate, row(norm_ffn_post[1]), S)
    return xt.reshape(B, S, D)
```
